```python
import jax, jax.numpy as jnp
from jax import lax
import numpy as np

D_MODEL = 2048
BATCH = 16
SEQ = 256
DEPTH = 2
DEC_BATCH = 4
DEC_SEQ = 4096
PAST_LEN = 256

GRID_W = 64
MLA_HEADS = 8
MLA_NOPE = 64
MLA_ROPE = 32
MLA_V = 64
Q_LORA = 512
KV_LORA = 256
ROPE_BASE = 10000.0
CONV_C = 512
CONV_K = 31
NA_HEADS = 8
NA_HD = 64
NA_KH = 8
NA_KW = 16
PEER_HEADS = 8
N_KEYS = 128
N_EXPERTS = N_KEYS * N_KEYS
PEER_TOPK = 16
PEER_QD = 256
PEER_CHUNK = 128
Q_BLOCK = 128
EPS = 1e-6
IN_SIZES = (Q_LORA, KV_LORA, MLA_ROPE, 2 * CONV_C, 3 * NA_HEADS * NA_HD, 3 * D_MODEL)
IN_COLS = Q_LORA + KV_LORA + MLA_ROPE + 2 * CONV_C + 3 * NA_HEADS * NA_HD + 3 * D_MODEL
IN_SPLITS = (Q_LORA, Q_LORA + KV_LORA, Q_LORA + KV_LORA + MLA_ROPE,
             Q_LORA + KV_LORA + MLA_ROPE + 2 * CONV_C,
             Q_LORA + KV_LORA + MLA_ROPE + 2 * CONV_C + 3 * NA_HEADS * NA_HD)

kernel_name = 'hybrid_mla_conformer_natten_peer_diffusion_step'

F32 = jnp.float32


def _rmsnorm(x, g):
    x32 = x.astype(F32)
    y = x32 * lax.rsqrt(jnp.mean(x32 * x32, axis=-1, keepdims=True) + EPS)
    return (y * g.astype(F32)).astype(x.dtype)


def _layernorm(x, g, b):
    x32 = x.astype(F32)
    mu = jnp.mean(x32, axis=-1, keepdims=True)
    var = jnp.mean(jnp.square(x32 - mu), axis=-1, keepdims=True)
    y = (x32 - mu) * lax.rsqrt(var + EPS)
    return (y * g.astype(F32) + b.astype(F32)).astype(x.dtype)


def _axial_rope(x):
    S = x.shape[1]
    half = x.shape[-1] // 2
    nf = half // 2
    freqs = 1.0 / (ROPE_BASE ** (jnp.arange(nf, dtype=F32) / nf))
    t = jnp.arange(S)
    rows = (t // GRID_W).astype(F32)
    cols = (t % GRID_W).astype(F32)
    shp = (1, S) + (1,) * (x.ndim - 3) + (nf,)

    def rot(xp, pos):
        ang = pos[:, None] * freqs[None, :]
        cos = jnp.cos(ang).reshape(shp).astype(x.dtype)
        sin = jnp.sin(ang).reshape(shp).astype(x.dtype)
        a, b = xp[..., :nf], xp[..., nf:]
        return jnp.concatenate([a * cos - b * sin, a * sin + b * cos], axis=-1)

    return jnp.concatenate([rot(x[..., :half], rows), rot(x[..., half:], cols)], axis=-1)


def _mla_attention(q_nope, q_rope, k_nope, k_rope, v):
    B, S, H, _ = q_nope.shape
    nb = S // Q_BLOCK
    scale = (MLA_NOPE + MLA_ROPE) ** -0.5

    def blocks(a):
        return a.reshape((B, nb, Q_BLOCK) + a.shape[2:]).swapaxes(0, 1)

    def step(qs):
        qn, qr = qs
        s = (jnp.einsum('bqhd,bkhd->bhqk', qn, k_nope)
             + jnp.einsum('bqhd,bkd->bhqk', qr, k_rope)).astype(F32) * scale
        p = jax.nn.softmax(s, axis=-1).astype(v.dtype)
        return jnp.einsum('bhqk,bkhe->bqhe', p, v)

    o = lax.map(step, (blocks(q_nope), blocks(q_rope)))
    return o.swapaxes(0, 1).reshape(B, S, H * v.shape[-1])


def _dense_attention(q, k, v):
    B, S, H, d = q.shape
    nb = S // Q_BLOCK
    qb = q.reshape(B, nb, Q_BLOCK, H, d).swapaxes(0, 1)

    def step(qq):
        s = jnp.einsum('bqhd,bkhd->bhqk', qq, k).astype(F32) * (d ** -0.5)
        p = jax.nn.softmax(s, axis=-1).astype(v.dtype)
        return jnp.einsum('bhqk,bkhe->bqhe', p, v)

    o = lax.map(step, qb)
    return o.swapaxes(0, 1).reshape(B, S, H * v.shape[-1])


def _na_latent(q, k, v, k_ctx, v_ctx, rpb):
    B, S, H, d = q.shape
    R = S // GRID_W
    KH = min(NA_KH, R)
    W = GRID_W
    scale = d ** -0.5
    qg = q.reshape(B, R, W, H, d)
    kg = k.reshape(B, R, W, H, d)
    vg = v.reshape(B, R, W, H, d)
    r = jnp.arange(R)
    row_idx = jnp.clip(r - KH // 2, 0, R - KH)[:, None] + jnp.arange(KH)[None, :]
    k_rows = kg[:, row_idx]
    v_rows = vg[:, row_idx]
    w = jnp.arange(W)
    col_start = jnp.clip(w - NA_KW // 2, 0, W - NA_KW)
    in_win = (w[None, :] >= col_start[:, None]) & (w[None, :] < col_start[:, None] + NA_KW)
    dr = row_idx - r[:, None] + (NA_KH - 1)
    dc = jnp.clip(w[None, :] - w[:, None], -(NA_KW - 1), NA_KW - 1) + (NA_KW - 1)
    bias = rpb[:, dr[:, :, None, None], dc[None, None, :, :]]
    bias = bias.transpose(1, 0, 3, 2, 4).astype(F32)
    s_loc = jnp.einsum('brwhd,brkuhd->brhwku', qg, k_rows).astype(F32) * scale + bias
    s_loc = jnp.where(in_win[:, None, :], s_loc, -jnp.inf)
    s_ctx = jnp.einsum('brwhd,bchd->brhwc', qg, k_ctx).astype(F32) * scale
    n_loc = KH * W
    s = jnp.concatenate([s_loc.reshape(B, R, H, W, n_loc), s_ctx], axis=-1)
    p = jax.nn.softmax(s, axis=-1).astype(v.dtype)
    p_loc = p[..., :n_loc].reshape(B, R, H, W, KH, W)
    p_ctx = p[..., n_loc:]
    o = (jnp.einsum('brhwku,brkuhd->brwhd', p_loc, v_rows)
         + jnp.einsum('brhwc,bchd->brwhd', p_ctx, v_ctx))
    return o.reshape(B, S, H * d)


def _conv_module(zc, w_dw, b_dw, g_ln, b_ln, w_pw2):
    a, b = jnp.split(zc, 2, axis=-1)
    u = a * jax.nn.sigmoid(b)
    u = lax.conv_general_dilated(u, w_dw[:, None, :], window_strides=(1,),
                                 padding=[(CONV_K // 2, CONV_K // 2)],
                                 dimension_numbers=('NWC', 'WIO', 'NWC'),
                                 feature_group_count=CONV_C) + b_dw
    u = jax.nn.silu(_layernorm(u, g_ln, b_ln))
    return u @ w_pw2


def _peer(h, w_pq, sub_k1, sub_k2, peer_u, peer_v):
    B, S, D = h.shape
    xs = h.reshape((B * S) // PEER_CHUNK, PEER_CHUNK, D)
    half = PEER_QD // 2

    def chunk(xc):
        q = (xc @ w_pq).reshape(PEER_CHUNK, PEER_HEADS, PEER_QD)
        s1 = jnp.einsum('thd,hnd->thn', q[..., :half], sub_k1).astype(F32)
        s2 = jnp.einsum('thd,hnd->thn', q[..., half:], sub_k2).astype(F32)
        v1, i1 = lax.top_k(s1, PEER_TOPK)
        v2, i2 = lax.top_k(s2, PEER_TOPK)
        cand = (v1[..., :, None] + v2[..., None, :]).reshape(PEER_CHUNK, PEER_HEADS, PEER_TOPK * PEER_TOPK)
        sv, si = lax.top_k(cand, PEER_TOPK)
        e1 = jnp.take_along_axis(i1, si // PEER_TOPK, axis=-1)
        e2 = jnp.take_along_axis(i2, si % PEER_TOPK, axis=-1)
        expert = e1 * N_KEYS + e2
        g = jax.nn.softmax(sv, axis=-1).astype(xc.dtype)
        act = jax.nn.gelu(jnp.einsum('td,thkd->thk', xc, peer_u[expert]))
        return jnp.einsum('thk,thkd->td', g * act, peer_v[expert])

    return lax.map(chunk, xs).reshape(B, S, D)


def _layer(x, mod, p, ctx=None):
    sh1, sc1, g1, sh2, sc2, g2 = jnp.split(mod, 6, axis=-1)
    B, S, _ = x.shape
    h = _rmsnorm(x, p['g_norm1']) * (1 + sc1) + sh1
    zq, zkv, zkr, zc, zna, zg = jnp.split(h @ p['w_in'], IN_SPLITS, axis=-1)
    q = (_rmsnorm(zq, p['g_q_a']) @ p['w_uq']).reshape(B, S, MLA_HEADS, MLA_NOPE + MLA_ROPE)
    q_nope, q_rope = q[..., :MLA_NOPE], q[..., MLA_NOPE:]
    ckv = _rmsnorm(zkv, p['g_kv_a'])
    qna, kna, vna = [t.reshape(B, S, NA_HEADS, NA_HD) for t in jnp.split(zna, 3, axis=-1)]
    if ctx is None:
        ckv_all, kr_all = ckv, zkr
        o_na = _dense_attention(qna, kna, vna)
        new = (ckv, zkr, kna, vna)
    else:
        ckv_c, kr_c, kna_c, vna_c = ctx
        q_rope = _axial_rope(q_rope)
        ckv_all = jnp.concatenate([ckv_c, ckv], axis=1)
        kr_all = jnp.concatenate([kr_c, _axial_rope(zkr)], axis=1)
        o_na = _na_latent(qna, kna, vna, kna_c, vna_c, p['rpb'])
        new = None
    T = ckv_all.shape[1]
    k_nope = (ckv_all @ p['w_uk']).reshape(B, T, MLA_HEADS, MLA_NOPE)
    v_mla = (ckv_all @ p['w_uv']).reshape(B, T, MLA_HEADS, MLA_V)
    y_a = _mla_attention(q_nope, q_rope, k_nope, kr_all, v_mla) @ p['w_o_mla']
    y_b = _conv_module(zc, p['w_dw'], p['b_dw'], p['g_conv_ln'], p['b_conv_ln'], p['w_pw2'])
    y_c = o_na @ p['w_o_na']
    ga, gb, gc = jnp.split(jax.nn.sigmoid(zg), 3, axis=-1)
    x = x + g1 * ((ga * y_a + gb * y_b + gc * y_c) @ p['w_out'])
    h2 = _rmsnorm(x, p['g_norm2']) * (1 + sc2) + sh2
    x = x + g2 * _peer(h2, p['w_pq'], p['sub_k1'], p['sub_k2'], p['peer_u'], p['peer_v'])
    return x, new


def setup_inputs(seed: int = 0) -> dict:
    key = jax.random.key(seed)
    ks = jax.random.split(key, 40)

    def nrm(k, shape, scale=1.0):
        return jax.random.normal(k, shape, dtype=F32) * scale

    def gain(k, shape):
        return 1.0 + 0.01 * jax.random.normal(k, shape, dtype=F32)

    D = D_MODEL
    NA_W = NA_HEADS * NA_HD
    return {
        'x_prompt': nrm(ks[0], (BATCH, SEQ, D)),
        'x_sample': nrm(ks[1], (DEC_BATCH, DEC_SEQ, D)),
        'cache_ckv': nrm(ks[2], (DEC_BATCH, DEPTH, PAST_LEN, KV_LORA)),
        'cache_krope': nrm(ks[3], (DEC_BATCH, DEPTH, PAST_LEN, MLA_ROPE)),
        'cache_na_k': nrm(ks[4], (DEC_BATCH, DEPTH, PAST_LEN, NA_HEADS, NA_HD)),
        'cache_na_v': nrm(ks[5], (DEC_BATCH, DEPTH, PAST_LEN, NA_HEADS, NA_HD)),
        'c': nrm(ks[6], (DEC_BATCH, D)),
        'c_ctx': nrm(ks[7], (D,)),
        'w_ada': nrm(ks[8], (DEPTH, D, 6 * D), 0.5 * D ** -0.5),
        'b_ada': nrm(ks[9], (DEPTH, 6 * D), 0.01),
        'g_norm1': gain(ks[10], (DEPTH, D)),
        'w_in': nrm(ks[11], (DEPTH, D, IN_COLS), D ** -0.5),
        'g_q_a': gain(ks[12], (DEPTH, Q_LORA)),
        'w_uq': nrm(ks[13], (DEPTH, Q_LORA, MLA_HEADS * (MLA_NOPE + MLA_ROPE)), Q_LORA ** -0.5),
        'g_kv_a': gain(ks[14], (DEPTH, KV_LORA)),
        'w_uk': nrm(ks[15], (DEPTH, KV_LORA, MLA_HEADS * MLA_NOPE), KV_LORA ** -0.5),
        'w_uv': nrm(ks[16], (DEPTH, KV_LORA, MLA_HEADS * MLA_V), KV_LORA ** -0.5),
        'w_o_mla': nrm(ks[17], (DEPTH, MLA_HEADS * MLA_V, D), (MLA_HEADS * MLA_V) ** -0.5),
        'w_dw': nrm(ks[18], (DEPTH, CONV_K, CONV_C), CONV_K ** -0.5),
        'b_dw': nrm(ks[19], (DEPTH, CONV_C), 0.01),
        'g_conv_ln': gain(ks[20], (DEPTH, CONV_C)),
        'b_conv_ln': nrm(ks[21], (DEPTH, CONV_C), 0.01),
        'w_pw2': nrm(ks[22], (DEPTH, CONV_C, D), CONV_C ** -0.5),
        'rpb': nrm(ks[23], (DEPTH, NA_HEADS, 2 * NA_KH - 1, 2 * NA_KW - 1), 0.1),
        'w_o_na': nrm(ks[24], (DEPTH, NA_W, D), NA_W ** -0.5),
        'w_out': nrm(ks[25], (DEPTH, D, D), D ** -0.5),
        'g_norm2': gain(ks[26], (DEPTH, D)),
        'w_pq': nrm(ks[27], (DEPTH, D, PEER_HEADS * PEER_QD), D ** -0.5),
        'sub_k1': nrm(ks[28], (DEPTH, PEER_HEADS, N_KEYS, PEER_QD // 2), (PEER_QD // 2) ** -0.5),
        'sub_k2': nrm(ks[29], (DEPTH, PEER_HEADS, N_KEYS, PEER_QD // 2), (PEER_QD // 2) ** -0.5),
        'peer_u': nrm(ks[30], (DEPTH, N_EXPERTS, D), D ** -0.5),
        'peer_v': nrm(ks[31], (DEPTH, N_EXPERTS, D), PEER_HEADS ** -0.5),
        'g_final': gain(ks[32], (D,)),
    }


def reference(x_prompt, x_sample, cache_ckv, cache_krope, cache_na_k, cache_na_v, c, c_ctx,
              w_ada, b_ada, g_norm1, w_in, g_q_a, w_uq, g_kv_a, w_uk, w_uv, w_o_mla,
              w_dw, b_dw, g_conv_ln, b_conv_ln, w_pw2, rpb, w_o_na, w_out,
              g_norm2, w_pq, sub_k1, sub_k2, peer_u, peer_v, g_final):
    xp, xs = x_prompt, x_sample
    ckv_l, kr_l, nak_l, nav_l = [], [], [], []
    for l in range(DEPTH):
        p = dict(g_norm1=g_norm1[l], w_in=w_in[l], g_q_a=g_q_a[l], w_uq=w_uq[l], g_kv_a=g_kv_a[l],
                 w_uk=w_uk[l], w_uv=w_uv[l], w_o_mla=w_o_mla[l], w_dw=w_dw[l], b_dw=b_dw[l],
                 g_conv_ln=g_conv_ln[l], b_conv_ln=b_conv_ln[l], w_pw2=w_pw2[l], rpb=rpb[l],
                 w_o_na=w_o_na[l], w_out=w_out[l], g_norm2=g_norm2[l], w_pq=w_pq[l],
                 sub_k1=sub_k1[l], sub_k2=sub_k2[l], peer_u=peer_u[l], peer_v=peer_v[l])
        mod_ctx = jax.nn.silu(c_ctx) @ w_ada[l] + b_ada[l]
        mod_lat = (jax.nn.silu(c) @ w_ada[l] + b_ada[l])[:, None, :]
        xp, new = _layer(xp, mod_ctx, p)
        ckv_l.append(new[0])
        kr_l.append(new[1])
        nak_l.append(new[2])
        nav_l.append(new[3])
        xs, _ = _layer(xs, mod_lat, p,
                       (cache_ckv[:, l], cache_krope[:, l], cache_na_k[:, l], cache_na_v[:, l]))
    y_prompt = _rmsnorm(xp, g_final)
    y_sample = _rmsnorm(xs, g_final)
    new_ckv = jnp.stack(ckv_l, axis=1)
    new_krope = jnp.stack(kr_l, axis=1)
    new_na_k = jnp.stack(nak_l, axis=1)
    new_na_v = jnp.stack(nav_l, axis=1)
    return (y_prompt, y_sample, new_ckv, new_krope, new_na_k, new_na_v)
```

```python
import functools

import jax
import jax.numpy as jnp
from jax import lax
from jax.experimental import pallas as pl
from jax.experimental.pallas import tpu as pltpu

F32 = jnp.float32
BF16 = jnp.bfloat16

D_MODEL = 2048
BATCH = 16
SEQ = 256
DEPTH = 2
DEC_BATCH = 4
DEC_SEQ = 4096
PAST_LEN = 256
GRID_W = 64
GRID_R = DEC_SEQ // GRID_W
MLA_HEADS = 8
MLA_NOPE = 64
MLA_ROPE = 32
MLA_V = 64
Q_LORA = 512
KV_LORA = 256
ROPE_BASE = 10000.0
CONV_C = 512
CONV_K = 31
NA_HEADS = 8
NA_HD = 64
NA_KH = 8
NA_KW = 16
PEER_HEADS = 8
N_KEYS = 128
N_EXPERTS = N_KEYS * N_KEYS
PEER_TOPK = 16
PEER_QD = 256
EPS = 1e-6

N_CTX_TOK = BATCH * SEQ
N_LAT_TOK = DEC_BATCH * DEC_SEQ
N_TOK = N_CTX_TOK + N_LAT_TOK
GROUP_ROWS = 4096
N_GROUPS = N_TOK // GROUP_ROWS
HEADS = 8
LANES = 128

Z_GATE = 0
Z_NAQ = 3 * D_MODEL
Z_Q = Z_NAQ + 3 * 512
Z_CONV = Z_Q + Q_LORA
Z_TAIL = Z_CONV + 2 * CONV_C
Z_COLS = Z_TAIL + 512
IN_TN = 512
N_SIG_BLOCKS = 3 * D_MODEL // IN_TN

VMEM_LIMIT = 56 * 1024 * 1024

NA_QROWS = 4
NA_WROWS = 12
NA_TQ = NA_QROWS * GRID_W
NA_TW = NA_WROWS * GRID_W


def _cparams(*sem):
    return pltpu.CompilerParams(dimension_semantics=sem, vmem_limit_bytes=VMEM_LIMIT)


def _rms(x, g):
    return x * lax.rsqrt(jnp.mean(x * x, axis=-1, keepdims=True) + EPS) * g


def _ada_kernel(c_ref, w_ref, b_ref, o_ref):
    cv = c_ref[...]
    a = (cv * jax.nn.sigmoid(cv)).astype(BF16)
    o_ref[...] = jnp.dot(a, w_ref[...].astype(BF16), preferred_element_type=F32) + b_ref[...]


def _ada(c8, w_ada, b_ada):
    tn = 1024
    n = w_ada.shape[-1]
    return pl.pallas_call(
        _ada_kernel,
        grid=(DEPTH, n // tn),
        in_specs=[
            pl.BlockSpec((8, D_MODEL), lambda l, j: (0, 0)),
            pl.BlockSpec((None, D_MODEL, tn), lambda l, j: (l, 0, j)),
            pl.BlockSpec((None, 1, tn), lambda l, j: (l, 0, j)),
        ],
        out_specs=pl.BlockSpec((None, 8, tn), lambda l, j: (l, 0, j)),
        out_shape=jax.ShapeDtypeStruct((DEPTH, 8, n), F32),
        compiler_params=_cparams("arbitrary", "arbitrary"),
        name="ada_mod",
    )(c8, w_ada, b_ada.reshape(DEPTH, 1, n))


def _inproj_kernel(x_ref, sh_ref, sc_ref, g_ref, w_ref, o_ref, h_scr):
    j = pl.program_id(1)

    @pl.when(j == 0)
    def _():
        h = _rms(x_ref[...], g_ref[...]) * (1.0 + sc_ref[...]) + sh_ref[...]
        h_scr[...] = h.astype(BF16)

    acc = jnp.dot(h_scr[...], w_ref[...], preferred_element_type=F32)

    @pl.when(j < N_SIG_BLOCKS)
    def _():
        o_ref[...] = jax.nn.sigmoid(acc).astype(BF16)

    @pl.when(j >= N_SIG_BLOCKS)
    def _():
        o_ref[...] = acc.astype(BF16)


def _mod_spec(chunk, tm):
    return pl.BlockSpec((None, None, 1, D_MODEL), lambda i, *_: (i * tm // GROUP_ROWS, chunk, 0, 0))


def _inproj(x, mod4, g_norm1, w_cat):
    tm = 1024
    return pl.pallas_call(
        _inproj_kernel,
        grid=(N_TOK // tm, Z_COLS // IN_TN),
        in_specs=[
            pl.BlockSpec((tm, D_MODEL), lambda i, j: (i, 0)),
            _mod_spec(0, tm),
            _mod_spec(1, tm),
            pl.BlockSpec((1, D_MODEL), lambda i, j: (0, 0)),
            pl.BlockSpec((D_MODEL, IN_TN), lambda i, j: (0, j)),
        ],
        out_specs=pl.BlockSpec((tm, IN_TN), lambda i, j: (i, j)),
        out_shape=jax.ShapeDtypeStruct((N_TOK, Z_COLS), BF16),
        scratch_shapes=[pltpu.VMEM((tm, D_MODEL), BF16)],
        compiler_params=_cparams("parallel", "arbitrary"),
        name="in_proj",
    )(x, mod4, mod4, g_norm1.reshape(1, D_MODEL), w_cat)


def _kv_expand(ckv, krp, wuk_ref, wuv_ref, k_out, v_out):
    cb = ckv.astype(BF16)
    kn = jnp.dot(cb, wuk_ref[...], preferred_element_type=F32)
    for h in range(HEADS):
        hs = slice(h * LANES, (h + 1) * LANES)
        k_out[:, hs] = (kn[:, hs] + krp).astype(BF16)
    v_out[...] = jnp.dot(cb, wuv_ref[...], preferred_element_type=F32).astype(BF16)


def _mla_prep_kernel(zq_ref, tail_ref, cos_ref, sin_ref, gq_ref, wq1_ref, wq2_ref, gkv_ref, wuk_ref, wuv_ref,
                     q_out, k_out, v_out, ckv_out):
    cos = cos_ref[...]
    sin = sin_ref[...]
    qn = _rms(zq_ref[...].astype(F32), gq_ref[...]).astype(BF16)
    q1 = jnp.dot(qn, wq1_ref[...], preferred_element_type=F32)
    q2 = jnp.dot(qn, wq2_ref[...], preferred_element_type=F32)
    scale = (MLA_NOPE + MLA_ROPE) ** -0.5
    for h in range(HEADS):
        hs = slice(h * LANES, (h + 1) * LANES)
        q_out[:, hs] = ((q1[:, hs] * cos + q2[:, hs] * sin) * scale).astype(BF16)
    tail = tail_ref[...].astype(F32)
    ckv = _rms(tail[:, :KV_LORA], gkv_ref[...])
    ckv_out[...] = ckv
    krp = tail[:, KV_LORA:KV_LORA + LANES] * cos + tail[:, KV_LORA + LANES:] * sin
    _kv_expand(ckv, krp, wuk_ref, wuv_ref, k_out, v_out)


def _mla_prep(z, cos_t, sin_t, g_q, wq1, wq2, g_kv, wuk, wuv):
    tm = 512
    nper = GROUP_ROWS // tm

    def tab(i):
        return (jnp.minimum(i // nper, 1), i % nper, 0)

    full = lambda a: pl.BlockSpec(a.shape, lambda i: (0,) * a.ndim)
    g_q = g_q.reshape(1, Q_LORA)
    g_kv = g_kv.reshape(1, KV_LORA)
    return pl.pallas_call(
        _mla_prep_kernel,
        grid=(N_TOK // tm,),
        in_specs=[
            pl.BlockSpec((tm, Q_LORA), lambda i: (i, Z_Q // Q_LORA)),
            pl.BlockSpec((tm, 512), lambda i: (i, Z_TAIL // 512)),
            pl.BlockSpec((None, tm, LANES), tab),
            pl.BlockSpec((None, tm, LANES), tab),
            full(g_q), full(wq1), full(wq2), full(g_kv), full(wuk), full(wuv),
        ],
        out_specs=[
            pl.BlockSpec((tm, HEADS * LANES), lambda i: (i, 0)),
            pl.BlockSpec((tm, HEADS * LANES), lambda i: (i, 0)),
            pl.BlockSpec((tm, HEADS * MLA_V), lambda i: (i, 0)),
            pl.BlockSpec((tm, KV_LORA), lambda i: (i, 0)),
        ],
        out_shape=[
            jax.ShapeDtypeStruct((N_TOK, HEADS * LANES), BF16),
            jax.ShapeDtypeStruct((N_TOK, HEADS * LANES), BF16),
            jax.ShapeDtypeStruct((N_TOK, HEADS * MLA_V), BF16),
            jax.ShapeDtypeStruct((N_TOK, KV_LORA), F32),
        ],
        compiler_params=_cparams("parallel"),
        name="mla_prep",
    )(z, z, cos_t, sin_t, g_q, wq1, wq2, g_kv, wuk, wuv)


def _cache_kv_kernel(ckv_ref, krp_ref, wuk_ref, wuv_ref, k_out, v_out):
    _kv_expand(ckv_ref[...], krp_ref[...], wuk_ref, wuv_ref, k_out, v_out)


def _cache_kv(ckv, krp, wuk, wuv):
    m = ckv.shape[0]
    full = lambda a: pl.BlockSpec(a.shape, lambda i: (0,) * a.ndim)
    return pl.pallas_call(
        _cache_kv_kernel,
        grid=(1,),
        in_specs=[full(ckv), full(krp), full(wuk), full(wuv)],
        out_specs=[pl.BlockSpec((m, HEADS * LANES), lambda i: (0, 0)),
                   pl.BlockSpec((m, HEADS * MLA_V), lambda i: (0, 0))],
        out_shape=[jax.ShapeDtypeStruct((m, HEADS * LANES), BF16),
                   jax.ShapeDtypeStruct((m, HEADS * MLA_V), BF16)],
        compiler_params=_cparams("arbitrary"),
        name="cache_kv",
    )(ckv, krp, wuk, wuv)


def _head_mask(h):
    lane = lax.broadcasted_iota(jnp.int32, (1, LANES), 1)
    return (lane < 64) if h % 2 == 0 else (lane >= 64)


def _attn_kernel(q_ref, k_ref, v_ref, o_ref, m_scr, l_scr, acc_scr, *, wide, nkv, qscale):
    kv = pl.program_id(2)

    @pl.when(kv == 0)
    def _():
        m_scr[...] = jnp.full(m_scr.shape, -jnp.inf, F32)
        l_scr[...] = jnp.zeros(l_scr.shape, F32)
        acc_scr[...] = jnp.zeros(acc_scr.shape, F32)

    for h in range(HEADS):
        pb = slice((h // 2) * LANES, (h // 2 + 1) * LANES)
        if wide:
            hs = slice(h * LANES, (h + 1) * LANES)
            qh = q_ref[:, hs]
            kh = k_ref[:, hs]
        else:
            qh = jnp.where(_head_mask(h), q_ref[:, pb] * qscale, 0.0).astype(BF16)
            kh = k_ref[:, pb]
        s = lax.dot_general(qh, kh, (((1,), (1,)), ((), ())), preferred_element_type=F32)
        m_prev = m_scr[h]
        m_new = jnp.maximum(m_prev, jnp.max(s, axis=-1, keepdims=True))
        alpha = jnp.exp(m_prev - m_new)
        p = jnp.exp(s - m_new[:, :1])
        l_scr[h] = alpha * l_scr[h] + jnp.sum(p, axis=-1, keepdims=True)
        acc_scr[h] = alpha * acc_scr[h] + jnp.dot(p.astype(BF16), v_ref[:, pb], preferred_element_type=F32)
        m_scr[h] = m_new

    @pl.when(kv == nkv - 1)
    def _():
        for pr in range(HEADS // 2):
            even = acc_scr[2 * pr] / l_scr[2 * pr]
            odd = acc_scr[2 * pr + 1] / l_scr[2 * pr + 1]
            o_ref[:, pr * LANES:(pr + 1) * LANES] = jnp.where(_head_mask(0), even, odd).astype(BF16)


def _attention(q, k, v, *, nb, boff, tq, tk, wide, qscale=1.0):
    s_len, wq = q.shape[1], q.shape[2]
    t_len = k.shape[1]
    nkv = t_len // tk
    kern = functools.partial(_attn_kernel, wide=wide, nkv=nkv, qscale=qscale)
    return pl.pallas_call(
        kern,
        grid=(nb, s_len // tq, nkv),
        in_specs=[
            pl.BlockSpec((None, tq, wq), lambda b, i, j: (b + boff, i, 0)),
            pl.BlockSpec((None, tk, wq), lambda b, i, j: (b, j, 0)),
            pl.BlockSpec((None, tk, 512), lambda b, i, j: (b, j, 0)),
        ],
        out_specs=pl.BlockSpec((None, tq, 512), lambda b, i, j: (b, i, 0)),
        out_shape=jax.ShapeDtypeStruct((nb, s_len, 512), BF16),
        scratch_shapes=[pltpu.VMEM((HEADS, tq, LANES), F32)] * 3,
        compiler_params=_cparams("parallel", "parallel", "arbitrary"),
        name="attn_wide" if wide else "attn_pair",
    )(q, k, v)


def _na_kernel(q_ref, k_ref, v_ref, kc_ref, vc_ref, bias_ref, o_ref):
    i = pl.program_id(1)
    first_row = jnp.clip(NA_QROWS * i - NA_KH // 2, 0, GRID_R - NA_WROWS)
    start = pl.multiple_of(first_row * GRID_W, GRID_W)
    kw = k_ref[pl.ds(start, NA_TW), :]
    vw = v_ref[pl.ds(start, NA_TW), :]
    scale = NA_HD ** -0.5
    nt = (((1,), (1,)), ((), ()))
    outs = []
    for h in range(HEADS):
        pb = slice((h // 2) * LANES, (h // 2 + 1) * LANES)
        qh = jnp.where(_head_mask(h), q_ref[:, pb] * scale, 0.0).astype(BF16)
        s_loc = lax.dot_general(qh, kw[:, pb], nt, preferred_element_type=F32) + bias_ref[h]
        s_ctx = lax.dot_general(qh, kc_ref[:, pb], nt, preferred_element_type=F32)
        m = jnp.maximum(jnp.max(s_loc, axis=-1, keepdims=True), jnp.max(s_ctx, axis=-1, keepdims=True))
        p_loc = jnp.exp(s_loc - m)
        p_ctx = jnp.exp(s_ctx - m)
        l = jnp.sum(p_loc, axis=-1, keepdims=True) + jnp.sum(p_ctx, axis=-1, keepdims=True)
        pv = (jnp.dot(p_loc.astype(BF16), vw[:, pb], preferred_element_type=F32)
              + jnp.dot(p_ctx.astype(BF16), vc_ref[:, pb], preferred_element_type=F32))
        outs.append(pv / l)
    for pr in range(HEADS // 2):
        o_ref[:, pr * LANES:(pr + 1) * LANES] = jnp.where(_head_mask(0), outs[2 * pr], outs[2 * pr + 1]).astype(BF16)


def _na_latent(z3, kc, vc, bias):
    nblk = GRID_R // NA_QROWS

    def cls(b, i):
        return (jnp.where(i == 0, 0, jnp.where(i == nblk - 1, 2, 1)), 0, 0, 0)

    return pl.pallas_call(
        _na_kernel,
        grid=(DEC_BATCH, nblk),
        in_specs=[
            pl.BlockSpec((None, NA_TQ, 512), lambda b, i: (b + 1, i, Z_NAQ // 512)),
            pl.BlockSpec((None, DEC_SEQ, 512), lambda b, i: (b + 1, 0, Z_NAQ // 512 + 1)),
            pl.BlockSpec((None, DEC_SEQ, 512), lambda b, i: (b + 1, 0, Z_NAQ // 512 + 2)),
            pl.BlockSpec((None, PAST_LEN, 512), lambda b, i: (b, 0, 0)),
            pl.BlockSpec((None, PAST_LEN, 512), lambda b, i: (b, 0, 0)),
            pl.BlockSpec((None, HEADS, NA_TQ, NA_TW), cls),
        ],
        out_specs=pl.BlockSpec((None, NA_TQ, 512), lambda b, i: (b, i, 0)),
        out_shape=jax.ShapeDtypeStruct((DEC_BATCH, DEC_SEQ, 512), BF16),
        compiler_params=_cparams("parallel", "arbitrary"),
        name="na_latent",
    )(z3, z3, z3, kc, vc, bias)


def _na_bias_table(rpb):
    a = jnp.arange(NA_QROWS)[:, None, None, None]
    w = jnp.arange(GRID_W)[None, :, None, None]
    j = jnp.arange(NA_WROWS)[None, None, :, None]
    u = jnp.arange(GRID_W)[None, None, None, :]
    col_start = jnp.clip(w - NA_KW // 2, 0, GRID_W - NA_KW)
    col_ok = (u >= col_start) & (u < col_start + NA_KW)
    dc = jnp.clip(u - w, -(NA_KW - 1), NA_KW - 1) + (NA_KW - 1)
    tabs = []
    nblk = GRID_R // NA_QROWS
    for blk in (0, 1, nblk - 1):
        r = NA_QROWS * blk + a
        first = min(max(NA_QROWS * blk - NA_KH // 2, 0), GRID_R - NA_WROWS)
        kr = first + j
        row_start = jnp.clip(r - NA_KH // 2, 0, GRID_R - NA_KH)
        ok = (kr >= row_start) & (kr < row_start + NA_KH) & col_ok
        dr = jnp.clip(kr - r + (NA_KH - 1), 0, 2 * NA_KH - 2)
        dr_b, dc_b = jnp.broadcast_arrays(dr, dc)
        bias = rpb[:, dr_b, dc_b].astype(F32)
        bias = jnp.where(ok[None], bias, -jnp.inf)
        tabs.append(bias.reshape(HEADS, NA_TQ, NA_TW))
    return jnp.stack(tabs)


CONV_PAD = 16
CONV_CH = 32


def _conv_kernel(zc_ref, wdw_ref, bdw_ref, gln_ref, bln_ref, o_ref, pad_scr, *, seq):
    pad_scr[0:CONV_PAD, :] = jnp.zeros((CONV_PAD, CONV_C), F32)
    pad_scr[CONV_PAD + seq:, :] = jnp.zeros((CONV_PAD, CONV_C), F32)
    glu_ch = 256

    def glu(c, carry):
        r0 = pl.multiple_of(c * glu_ch, glu_ch)
        zc = zc_ref[pl.ds(r0, glu_ch), :].astype(F32)
        pad_scr[pl.ds(CONV_PAD + r0, glu_ch), :] = zc[:, :CONV_C] * jax.nn.sigmoid(zc[:, CONV_C:])
        return carry

    lax.fori_loop(0, seq // glu_ch, glu, 0)

    def conv(c, carry):
        r0 = pl.multiple_of(c * CONV_CH, CONV_CH)
        acc = jnp.zeros((CONV_CH, CONV_C), F32) + bdw_ref[...]
        win = pad_scr[pl.ds(r0, CONV_CH + 2 * CONV_PAD), :]
        for t in range(CONV_K):
            lo = CONV_PAD - CONV_K // 2 + t
            acc = acc + wdw_ref[t:t + 1, :] * win[lo:lo + CONV_CH, :]
        mu = jnp.mean(acc, axis=-1, keepdims=True)
        d = acc - mu
        var = jnp.mean(d * d, axis=-1, keepdims=True)
        y = d * lax.rsqrt(var + EPS) * gln_ref[...] + bln_ref[...]
        o_ref[pl.ds(r0, CONV_CH), :] = (y * jax.nn.sigmoid(y)).astype(BF16)
        return carry

    lax.fori_loop(0, seq // CONV_CH, conv, 0)


def _conv(zv, boff, nseq, seq, w_dw, b_dw, g_ln, b_ln):
    full = lambda a: pl.BlockSpec(a.shape, lambda s: (0,) * a.ndim)
    vec = lambda a: a.reshape(1, CONV_C)
    return pl.pallas_call(
        functools.partial(_conv_kernel, seq=seq),
        grid=(nseq,),
        in_specs=[
            pl.BlockSpec((None, seq, 2 * CONV_C), lambda s: (s + boff, 0, Z_CONV // (2 * CONV_C))),
            full(w_dw), full(vec(b_dw)), full(vec(g_ln)), full(vec(b_ln)),
        ],
        out_specs=pl.BlockSpec((None, seq, CONV_C), lambda s: (s, 0, 0)),
        out_shape=jax.ShapeDtypeStruct((nseq, seq, CONV_C), BF16),
        scratch_shapes=[pltpu.VMEM((seq + 2 * CONV_PAD, CONV_C), F32)],
        compiler_params=_cparams("parallel"),
        name="conv_module",
    )(zv, w_dw, vec(b_dw), vec(g_ln), vec(b_ln))


def _merge_kernel(x_ref, ga_ref, gb_ref, gc_ref, oa_ref, ub_ref, oc_ref, woa_ref, wpw_ref, woc_ref, wout_ref,
                  g1_ref, gn2_ref, sh2_ref, sc2_ref, x_out, h2_out):
    ya = jnp.dot(oa_ref[...], woa_ref[...], preferred_element_type=F32)
    yb = jnp.dot(ub_ref[...], wpw_ref[...], preferred_element_type=F32)
    yc = jnp.dot(oc_ref[...], woc_ref[...], preferred_element_type=F32)
    mix = ga_ref[...].astype(F32) * ya + gb_ref[...].astype(F32) * yb + gc_ref[...].astype(F32) * yc
    x = x_ref[...] + g1_ref[...] * jnp.dot(mix.astype(BF16), wout_ref[...], preferred_element_type=F32)
    x_out[...] = x
    h2_out[...] = (_rms(x, gn2_ref[...]) * (1.0 + sc2_ref[...]) + sh2_ref[...]).astype(BF16)


def _merge(x, z, o_mla, u_conv, o_na, w_o_mla, w_pw2, w_o_na, w_out, mod4, g_norm2):
    tm = 256
    full = lambda a: pl.BlockSpec(a.shape, lambda i: (0,) * a.ndim)
    row = lambda w: pl.BlockSpec((tm, w), lambda i: (i, 0))
    gate = lambda k: pl.BlockSpec((tm, D_MODEL), lambda i: (i, k))
    g_norm2 = g_norm2.reshape(1, D_MODEL)
    return pl.pallas_call(
        _merge_kernel,
        grid=(N_TOK // tm,),
        in_specs=[
            row(D_MODEL), gate(0), gate(1), gate(2), row(512), row(512), row(512),
            full(w_o_mla), full(w_pw2), full(w_o_na), full(w_out),
            _mod_spec(2, tm), full(g_norm2), _mod_spec(3, tm), _mod_spec(4, tm),
        ],
        out_specs=[row(D_MODEL), row(D_MODEL)],
        out_shape=[jax.ShapeDtypeStruct((N_TOK, D_MODEL), F32), jax.ShapeDtypeStruct((N_TOK, D_MODEL), BF16)],
        compiler_params=_cparams("parallel"),
        name="merge_out",
    )(x, z, z, z, o_mla, u_conv, o_na, w_o_mla, w_pw2, w_o_na, w_out, mod4, g_norm2, mod4, mod4)


STAIR = [(a, PEER_TOPK // (a + 1)) for a in range(PEER_TOPK)]
N_CAND = sum(n for _, n in STAIR)
N_CAND_PAD = -(-N_CAND // 8) * 8


def _top_rows(s, out_scr):
    for k in range(PEER_TOPK):
        m = jnp.max(s, axis=0, keepdims=True)
        out_scr[k:k + 1, :] = m
        s = jnp.where(s == m, -jnp.inf, s)


def _peer_q_kernel(h2_ref, wpq_ref, k1_ref, k2_ref, s1_ref, s2_ref, w1_ref, w2_ref, tau_ref,
                   v1_scr, v2_scr, cand_scr, sv_scr):
    qp = jnp.dot(h2_ref[...], wpq_ref[...], preferred_element_type=F32).astype(BF16)
    nt = (((1,), (1,)), ((), ()))
    half = PEER_QD // 2
    cand_scr[N_CAND:, :] = jnp.full((N_CAND_PAD - N_CAND, cand_scr.shape[1]), -jnp.inf, F32)
    for h in range(PEER_HEADS):
        q1 = qp[:, h * PEER_QD:h * PEER_QD + half]
        q2 = qp[:, h * PEER_QD + half:(h + 1) * PEER_QD]
        s1 = lax.dot_general(k1_ref[h], q1, nt, preferred_element_type=F32)
        s2 = lax.dot_general(k2_ref[h], q2, nt, preferred_element_type=F32)
        _top_rows(s1, v1_scr)
        _top_rows(s2, v2_scr)
        off = 0
        for a, n in STAIR:
            cand_scr[off:off + n, :] = v1_scr[a:a + 1, :] + v2_scr[0:n, :]
            off += n
        _top_rows(cand_scr[...], sv_scr)
        sv = sv_scr[...]
        zsum = jnp.sum(jnp.exp(sv - sv[0:1, :]), axis=0, keepdims=True)
        s1_ref[h] = s1
        s2_ref[h] = s2
        w1_ref[h] = jnp.exp(s1 - v1_scr[0:1, :]) / zsum
        w2_ref[h] = jnp.exp(s2 - v2_scr[0:1, :])
        tau_ref[h:h + 1, :] = sv[PEER_TOPK - 1:PEER_TOPK, :]


def _peer_q(h2, w_pq, k1, k2):
    tm = 256
    full = lambda a: pl.BlockSpec(a.shape, lambda i: (0,) * a.ndim)
    keyed = pl.BlockSpec((PEER_HEADS, N_KEYS, tm), lambda i: (0, 0, i))
    keyed_shape = jax.ShapeDtypeStruct((PEER_HEADS, N_KEYS, N_TOK), F32)
    return pl.pallas_call(
        _peer_q_kernel,
        grid=(N_TOK // tm,),
        in_specs=[pl.BlockSpec((tm, D_MODEL), lambda i: (i, 0)), full(w_pq), full(k1), full(k2)],
        out_specs=[keyed, keyed, keyed, keyed, pl.BlockSpec((PEER_HEADS, tm), lambda i: (0, i))],
        out_shape=[keyed_shape] * 4 + [jax.ShapeDtypeStruct((PEER_HEADS, N_TOK), F32)],
        scratch_shapes=[pltpu.VMEM((PEER_TOPK, tm), F32), pltpu.VMEM((PEER_TOPK, tm), F32),
                        pltpu.VMEM((N_CAND_PAD, tm), F32), pltpu.VMEM((PEER_TOPK, tm), F32)],
        compiler_params=_cparams("parallel"),
        name="peer_query",
    )(h2, w_pq, k1, k2)


PEER_TM = 512
PEER_E1 = 4
PEER_EB = PEER_E1 * N_KEYS


def _gelu_tanh(x):
    return x * (0.5 * (1.0 + jnp.tanh(0.7978845608028654 * (x + 0.044715 * (x * x * x)))))


def _peer_kernel(h2_ref, u_ref, vt_ref, s1_ref, w1_ref, s2_ref, w2_ref, tau_ref, x_ref, g2_ref, x_out,
                 acc_scr, gate_scr):
    j = pl.program_id(1)

    @pl.when(j == 0)
    def _():
        acc_scr[...] = jnp.zeros(acc_scr.shape, F32)

    for e in range(PEER_E1):
        for c in range(PEER_TM // LANES):
            cs = slice(c * LANES, (c + 1) * LANES)
            g = jnp.zeros((N_KEYS, LANES), F32)
            for h in range(PEER_HEADS):
                cand = s2_ref[h, :, cs] + s1_ref[h, e, :, cs]
                g = g + jnp.where(cand >= tau_ref[h:h + 1, cs], w2_ref[h, :, cs] * w1_ref[h, e, :, cs], 0.0)
            gate_scr[e * N_KEYS:(e + 1) * N_KEYS, cs] = g

    st = lax.dot_general(u_ref[...], h2_ref[...], (((1,), (1,)), ((), ())), preferred_element_type=F32)
    act = (_gelu_tanh(st) * gate_scr[...]).astype(BF16)
    acc_scr[...] += jnp.dot(vt_ref[...], act, preferred_element_type=F32)

    @pl.when(j == pl.num_programs(1) - 1)
    def _():
        x_out[...] = x_ref[...] + g2_ref[...] * acc_scr[...].T


def _peer(h2, u, vt, s1, w1, s2, w2, tau, x, mod4):
    tm = PEER_TM
    e1spec = pl.BlockSpec((PEER_HEADS, PEER_E1, 1, tm), lambda i, j: (0, j, 0, i))
    s1 = s1.reshape(PEER_HEADS, N_KEYS, 1, N_TOK)
    w1 = w1.reshape(PEER_HEADS, N_KEYS, 1, N_TOK)
    keyed = pl.BlockSpec((PEER_HEADS, N_KEYS, tm), lambda i, j: (0, 0, i))
    return pl.pallas_call(
        _peer_kernel,
        grid=(N_TOK // tm, N_KEYS // PEER_E1),
        in_specs=[
            pl.BlockSpec((tm, D_MODEL), lambda i, j: (i, 0)),
            pl.BlockSpec((PEER_EB, D_MODEL), lambda i, j: (j, 0)),
            pl.BlockSpec((D_MODEL, PEER_EB), lambda i, j: (0, j)),
            e1spec, e1spec, keyed, keyed,
            pl.BlockSpec((PEER_HEADS, tm), lambda i, j: (0, i)),
            pl.BlockSpec((tm, D_MODEL), lambda i, j: (i, 0)),
            _mod_spec(5, tm),
        ],
        out_specs=pl.BlockSpec((tm, D_MODEL), lambda i, j: (i, 0)),
        out_shape=jax.ShapeDtypeStruct((N_TOK, D_MODEL), F32),
        scratch_shapes=[pltpu.VMEM((D_MODEL, tm), F32), pltpu.VMEM((PEER_EB, tm), F32)],
        compiler_params=_cparams("parallel", "arbitrary"),
        name="peer_experts",
    )(h2, u, vt, s1, w1, s2, w2, tau, x, mod4)


def _final_kernel(x_ref, g_ref, o_ref):
    o_ref[...] = _rms(x_ref[...], g_ref[...])


def _final_norm(x, g):
    tm = 1024
    return pl.pallas_call(
        _final_kernel,
        grid=(N_TOK // tm,),
        in_specs=[pl.BlockSpec((tm, D_MODEL), lambda i: (i, 0)), pl.BlockSpec((1, D_MODEL), lambda i: (0, 0))],
        out_specs=pl.BlockSpec((tm, D_MODEL), lambda i: (i, 0)),
        out_shape=jax.ShapeDtypeStruct((N_TOK, D_MODEL), F32),
        compiler_params=_cparams("parallel"),
        name="final_norm",
    )(x, g.reshape(1, D_MODEL))


def _rot_cols(w):
    q = MLA_ROPE // 4
    return jnp.concatenate([-w[:, q:2 * q], w[:, 0:q], -w[:, 3 * q:4 * q], w[:, 2 * q:3 * q]], axis=1)


def _rope_tables():
    nf = MLA_ROPE // 4
    freqs = 1.0 / (ROPE_BASE ** (jnp.arange(nf, dtype=F32) / nf))
    t = jnp.arange(DEC_SEQ)
    ang_r = (t // GRID_W).astype(F32)[:, None] * freqs[None, :]
    ang_c = (t % GRID_W).astype(F32)[:, None] * freqs[None, :]
    cos32 = jnp.concatenate([jnp.cos(ang_r)] * 2 + [jnp.cos(ang_c)] * 2, axis=1)
    sin32 = jnp.concatenate([jnp.sin(ang_r)] * 2 + [jnp.sin(ang_c)] * 2, axis=1)
    ones = jnp.ones((DEC_SEQ, LANES), F32)
    zeros = jnp.zeros((DEC_SEQ, LANES), F32)
    cos = ones.at[:, MLA_NOPE:MLA_NOPE + MLA_ROPE].set(cos32)
    sin = zeros.at[:, MLA_NOPE:MLA_NOPE + MLA_ROPE].set(sin32)
    return jnp.stack([ones, cos]), jnp.stack([zeros, sin])


def _in_rope_block(w_kr):
    return jnp.zeros((D_MODEL, LANES), F32).at[:, MLA_NOPE:MLA_NOPE + MLA_ROPE].set(w_kr)


def _layout_w_in(w):
    zq, zkv, zkr, zc, zna, zg = jnp.split(w, [512, 768, 800, 1824, 3360], axis=1)
    cat = jnp.concatenate([zg, zna, zq, zc, zkv, _in_rope_block(zkr), _in_rope_block(_rot_cols(zkr))], axis=1)
    return cat.astype(BF16)


def _layout_w_uq(w):
    w = w.reshape(Q_LORA, MLA_HEADS, MLA_NOPE + MLA_ROPE)
    pad = jnp.zeros((Q_LORA, MLA_HEADS, LANES - MLA_NOPE - MLA_ROPE), F32)
    w1 = jnp.concatenate([w, pad], axis=2)
    rot = jnp.stack([_rot_cols(w[:, h, MLA_NOPE:]) for h in range(MLA_HEADS)], axis=1)
    w2 = jnp.concatenate([jnp.zeros((Q_LORA, MLA_HEADS, MLA_NOPE), F32), rot, pad], axis=2)
    return w1.reshape(Q_LORA, -1).astype(BF16), w2.reshape(Q_LORA, -1).astype(BF16)


def _layout_w_uk(w):
    w = w.reshape(KV_LORA, MLA_HEADS, MLA_NOPE)
    w = jnp.concatenate([w, jnp.zeros((KV_LORA, MLA_HEADS, LANES - MLA_NOPE), F32)], axis=2)
    return w.reshape(KV_LORA, -1).astype(BF16)


def kernel(x_prompt, x_sample, cache_ckv, cache_krope, cache_na_k, cache_na_v, c, c_ctx, w_ada, b_ada, g_norm1,
           w_in, g_q_a, w_uq, g_kv_a, w_uk, w_uv, w_o_mla, w_dw, b_dw, g_conv_ln, b_conv_ln, w_pw2, rpb, w_o_na,
           w_out, g_norm2, w_pq, sub_k1, sub_k2, peer_u, peer_v, g_final):
    x = jnp.concatenate([x_prompt.reshape(N_CTX_TOK, D_MODEL), x_sample.reshape(N_LAT_TOK, D_MODEL)], axis=0)
    c8 = jnp.concatenate([c_ctx[None, :], c, jnp.zeros((8 - 1 - DEC_BATCH, D_MODEL), F32)], axis=0)
    mod = _ada(c8, w_ada, b_ada)
    cos_t, sin_t = _rope_tables()
    new_ckv, new_kr, new_nak, new_nav = [], [], [], []
    for l in range(DEPTH):
        mod4 = mod[l].reshape(8, 6, 1, D_MODEL)
        z = _inproj(x, mod4, g_norm1[l], _layout_w_in(w_in[l]))
        z3 = z.reshape(N_GROUPS, GROUP_ROWS, Z_COLS)
        zs = z.reshape(N_TOK // SEQ, SEQ, Z_COLS)

        wq1, wq2 = _layout_w_uq(w_uq[l])
        wuk = _layout_w_uk(w_uk[l])
        wuv = w_uv[l].astype(BF16)
        q, k, v, ckv = _mla_prep(z, cos_t, sin_t, g_q_a[l], wq1, wq2, g_kv_a[l], wuk, wuv)
        krp_c = jnp.zeros((DEC_BATCH * PAST_LEN, LANES), F32).at[:, MLA_NOPE:MLA_NOPE + MLA_ROPE].set(
            cache_krope[:, l].reshape(-1, MLA_ROPE))
        k_c, v_c = _cache_kv(cache_ckv[:, l].reshape(-1, KV_LORA), krp_c, wuk, wuv)
        k3 = k.reshape(N_GROUPS, GROUP_ROWS, -1)
        v3 = v.reshape(N_GROUPS, GROUP_ROWS, -1)
        k_all = jnp.concatenate([k_c.reshape(DEC_BATCH, PAST_LEN, -1), k3[1:]], axis=1)
        v_all = jnp.concatenate([v_c.reshape(DEC_BATCH, PAST_LEN, -1), v3[1:]], axis=1)
        o_lat = _attention(q.reshape(N_GROUPS, GROUP_ROWS, -1), k_all, v_all, nb=DEC_BATCH, boff=1,
                           tq=512, tk=(PAST_LEN + DEC_SEQ) // 2, wide=True)
        o_ctx = _attention(q.reshape(N_TOK // SEQ, SEQ, -1), k.reshape(N_TOK // SEQ, SEQ, -1),
                           v.reshape(N_TOK // SEQ, SEQ, -1), nb=BATCH, boff=0, tq=SEQ, tk=SEQ, wide=True)
        o_mla = jnp.concatenate([o_ctx.reshape(N_CTX_TOK, 512), o_lat.reshape(N_LAT_TOK, 512)], axis=0)

        u_ctx = _conv(zs, 0, BATCH, SEQ, w_dw[l], b_dw[l], g_conv_ln[l], b_conv_ln[l])
        u_lat = _conv(z3, 1, DEC_BATCH, DEC_SEQ, w_dw[l], b_dw[l], g_conv_ln[l], b_conv_ln[l])
        u_conv = jnp.concatenate([u_ctx.reshape(N_CTX_TOK, CONV_C), u_lat.reshape(N_LAT_TOK, CONV_C)], axis=0)

        naq = z[:, Z_NAQ:Z_NAQ + 512].reshape(N_TOK // SEQ, SEQ, 512)
        nak = z[:N_CTX_TOK, Z_NAQ + 512:Z_NAQ + 1024].reshape(BATCH, SEQ, 512)
        nav = z[:N_CTX_TOK, Z_NAQ + 1024:Z_NAQ + 1536].reshape(BATCH, SEQ, 512)
        on_ctx = _attention(naq, nak, nav, nb=BATCH, boff=0, tq=SEQ, tk=SEQ, wide=False, qscale=NA_HD ** -0.5)
        on_lat = _na_latent(z3, cache_na_k[:, l].reshape(DEC_BATCH, PAST_LEN, 512).astype(BF16),
                            cache_na_v[:, l].reshape(DEC_BATCH, PAST_LEN, 512).astype(BF16), _na_bias_table(rpb[l]))
        o_na = jnp.concatenate([on_ctx.reshape(N_CTX_TOK, 512), on_lat.reshape(N_LAT_TOK, 512)], axis=0)

        x, h2 = _merge(x, z, o_mla, u_conv, o_na, w_o_mla[l].astype(BF16), w_pw2[l].astype(BF16),
                       w_o_na[l].astype(BF16), w_out[l].astype(BF16), mod4, g_norm2[l])

        s1, s2, w1, w2, tau = _peer_q(h2, w_pq[l].astype(BF16), sub_k1[l].astype(BF16), sub_k2[l].astype(BF16))
        x = _peer(h2, peer_u[l].astype(BF16), peer_v[l].T.astype(BF16), s1, w1, s2, w2, tau, x, mod4)

        new_ckv.append(ckv[:N_CTX_TOK].reshape(BATCH, SEQ, KV_LORA))
        kr = z[:N_CTX_TOK, Z_TAIL + KV_LORA + MLA_NOPE:Z_TAIL + KV_LORA + MLA_NOPE + MLA_ROPE]
        new_kr.append(kr.astype(F32).reshape(BATCH, SEQ, MLA_ROPE))
        new_nak.append(nak.astype(F32).reshape(BATCH, SEQ, NA_HEADS, NA_HD))
        new_nav.append(nav.astype(F32).reshape(BATCH, SEQ, NA_HEADS, NA_HD))

    y = _final_norm(x, g_final)
    return (y[:N_CTX_TOK].reshape(BATCH, SEQ, D_MODEL), y[N_CTX_TOK:].reshape(DEC_BATCH, DEC_SEQ, D_MODEL),
            jnp.stack(new_ckv, axis=1), jnp.stack(new_kr, axis=1), jnp.stack(new_nak, axis=1),
            jnp.stack(new_nav, axis=1))
```

```python
import functools

import jax
import jax.numpy as jnp
from jax import lax
from jax.experimental import pallas as pl
from jax.experimental.pallas import tpu as pltpu

F32 = jnp.float32
BF16 = jnp.bfloat16

D_MODEL = 2048
BATCH = 16
SEQ = 256
DEPTH = 2
DEC_BATCH = 4
DEC_SEQ = 4096
PAST_LEN = 256
GRID_W = 64
GRID_R = DEC_SEQ // GRID_W
MLA_HEADS = 8
MLA_NOPE = 64
MLA_ROPE = 32
MLA_V = 64
Q_LORA = 512
KV_LORA = 256
ROPE_BASE = 10000.0
CONV_C = 512
CONV_K = 31
NA_HEADS = 8
NA_HD = 64
NA_KH = 8
NA_KW = 16
PEER_HEADS = 8
N_KEYS = 128
N_EXPERTS = N_KEYS * N_KEYS
PEER_TOPK = 16
PEER_QD = 256
EPS = 1e-6

N_CTX_TOK = BATCH * SEQ
N_LAT_TOK = DEC_BATCH * DEC_SEQ
N_TOK = N_CTX_TOK + N_LAT_TOK
GROUP_ROWS = 4096
N_GROUPS = N_TOK // GROUP_ROWS
HEADS = 8
LANES = 128

Z_GATE = 0
Z_NAQ = 3 * D_MODEL
Z_Q = Z_NAQ + 3 * 512
Z_CONV = Z_Q + Q_LORA
Z_TAIL = Z_CONV + 2 * CONV_C
Z_COLS = Z_TAIL + 512
IN_TN = 512
N_SIG_BLOCKS = 3 * D_MODEL // IN_TN

VMEM_LIMIT = 56 * 1024 * 1024

NA_QROWS = 4
NA_WROWS = 12
NA_TQ = NA_QROWS * GRID_W
NA_TW = NA_WROWS * GRID_W


def _cparams(*sem, flags=None):
    return pltpu.CompilerParams(dimension_semantics=sem, vmem_limit_bytes=VMEM_LIMIT, flags=flags)


def _rms(x, g):
    return x * lax.rsqrt(jnp.mean(x * x, axis=-1, keepdims=True) + EPS) * g


def _ada_kernel(c_ref, w_ref, b_ref, o_ref):
    cv = c_ref[...]
    a = (cv * jax.nn.sigmoid(cv)).astype(BF16)
    o_ref[...] = jnp.dot(a, w_ref[...].astype(BF16), preferred_element_type=F32) + b_ref[...]


def _ada(c8, w_ada, b_ada):
    tn = 1024
    n = w_ada.shape[-1]
    return pl.pallas_call(
        _ada_kernel,
        grid=(DEPTH, n // tn),
        in_specs=[
            pl.BlockSpec((8, D_MODEL), lambda l, j: (0, 0)),
            pl.BlockSpec((None, D_MODEL, tn), lambda l, j: (l, 0, j)),
            pl.BlockSpec((None, 1, tn), lambda l, j: (l, 0, j)),
        ],
        out_specs=pl.BlockSpec((None, 8, tn), lambda l, j: (l, 0, j)),
        out_shape=jax.ShapeDtypeStruct((DEPTH, 8, n), F32),
        compiler_params=_cparams("arbitrary", "arbitrary"),
        name="ada_mod",
    )(c8, w_ada, b_ada.reshape(DEPTH, 1, n))


def _inproj_kernel(x_ref, sh_ref, sc_ref, g_ref, w_ref, o_ref, h_scr):
    j = pl.program_id(1)

    @pl.when(j == 0)
    def _():
        h = _rms(x_ref[...], g_ref[...]) * (1.0 + sc_ref[...]) + sh_ref[...]
        h_scr[...] = h.astype(BF16)

    acc = jnp.dot(h_scr[...], w_ref[...], preferred_element_type=F32)

    @pl.when(j < N_SIG_BLOCKS)
    def _():
        o_ref[...] = jax.nn.sigmoid(acc).astype(BF16)

    @pl.when(j >= N_SIG_BLOCKS)
    def _():
        o_ref[...] = acc.astype(BF16)


def _mod_spec(chunk, tm):
    return pl.BlockSpec((None, None, 1, D_MODEL), lambda i, *_: (i * tm // GROUP_ROWS, chunk, 0, 0))


def _inproj(x, mod4, g_norm1, w_cat):
    tm = 1024
    return pl.pallas_call(
        _inproj_kernel,
        grid=(N_TOK // tm, Z_COLS // IN_TN),
        in_specs=[
            pl.BlockSpec((tm, D_MODEL), lambda i, j: (i, 0)),
            _mod_spec(0, tm),
            _mod_spec(1, tm),
            pl.BlockSpec((1, D_MODEL), lambda i, j: (0, 0)),
            pl.BlockSpec((D_MODEL, IN_TN), lambda i, j: (0, j)),
        ],
        out_specs=pl.BlockSpec((tm, IN_TN), lambda i, j: (i, j)),
        out_shape=jax.ShapeDtypeStruct((N_TOK, Z_COLS), BF16),
        scratch_shapes=[pltpu.VMEM((tm, D_MODEL), BF16)],
        compiler_params=_cparams("parallel", "arbitrary"),
        name="in_proj",
    )(x, mod4, mod4, g_norm1.reshape(1, D_MODEL), w_cat)


def _kv_expand(ckv, krp, wuk_ref, wuv_ref, k_out, v_out):
    cb = ckv.astype(BF16)
    kn = jnp.dot(cb, wuk_ref[...], preferred_element_type=F32)
    for h in range(HEADS):
        hs = slice(h * LANES, (h + 1) * LANES)
        k_out[:, hs] = (kn[:, hs] + krp).astype(BF16)
    v_out[...] = jnp.dot(cb, wuv_ref[...], preferred_element_type=F32).astype(BF16)


def _mla_prep_kernel(zq_ref, tail_ref, cos_ref, sin_ref, gq_ref, wq1_ref, wq2_ref, gkv_ref, wuk_ref, wuv_ref,
                     q_out, k_out, v_out, ckv_out):
    cos = cos_ref[...]
    sin = sin_ref[...]
    qn = _rms(zq_ref[...].astype(F32), gq_ref[...]).astype(BF16)
    q1 = jnp.dot(qn, wq1_ref[...], preferred_element_type=F32)
    q2 = jnp.dot(qn, wq2_ref[...], preferred_element_type=F32)
    scale = (MLA_NOPE + MLA_ROPE) ** -0.5
    for h in range(HEADS):
        hs = slice(h * LANES, (h + 1) * LANES)
        q_out[:, hs] = ((q1[:, hs] * cos + q2[:, hs] * sin) * scale).astype(BF16)
    tail = tail_ref[...].astype(F32)
    ckv = _rms(tail[:, :KV_LORA], gkv_ref[...])
    ckv_out[...] = ckv
    krp = tail[:, KV_LORA:KV_LORA + LANES] * cos + tail[:, KV_LORA + LANES:] * sin
    _kv_expand(ckv, krp, wuk_ref, wuv_ref, k_out, v_out)


def _mla_prep(z, cos_t, sin_t, g_q, wq1, wq2, g_kv, wuk, wuv):
    tm = 512
    nper = GROUP_ROWS // tm

    def tab(i):
        return (jnp.minimum(i // nper, 1), i % nper, 0)

    full = lambda a: pl.BlockSpec(a.shape, lambda i: (0,) * a.ndim)
    g_q = g_q.reshape(1, Q_LORA)
    g_kv = g_kv.reshape(1, KV_LORA)
    return pl.pallas_call(
        _mla_prep_kernel,
        grid=(N_TOK // tm,),
        in_specs=[
            pl.BlockSpec((tm, Q_LORA), lambda i: (i, Z_Q // Q_LORA)),
            pl.BlockSpec((tm, 512), lambda i: (i, Z_TAIL // 512)),
            pl.BlockSpec((None, tm, LANES), tab),
            pl.BlockSpec((None, tm, LANES), tab),
            full(g_q), full(wq1), full(wq2), full(g_kv), full(wuk), full(wuv),
        ],
        out_specs=[
            pl.BlockSpec((tm, HEADS * LANES), lambda i: (i, 0)),
            pl.BlockSpec((tm, HEADS * LANES), lambda i: (i, 0)),
            pl.BlockSpec((tm, HEADS * MLA_V), lambda i: (i, 0)),
            pl.BlockSpec((tm, KV_LORA), lambda i: (i, 0)),
        ],
        out_shape=[
            jax.ShapeDtypeStruct((N_TOK, HEADS * LANES), BF16),
            jax.ShapeDtypeStruct((N_TOK, HEADS * LANES), BF16),
            jax.ShapeDtypeStruct((N_TOK, HEADS * MLA_V), BF16),
            jax.ShapeDtypeStruct((N_TOK, KV_LORA), F32),
        ],
        compiler_params=_cparams("parallel"),
        name="mla_prep",
    )(z, z, cos_t, sin_t, g_q, wq1, wq2, g_kv, wuk, wuv)


def _cache_kv_kernel(ckv_ref, krp_ref, wuk_ref, wuv_ref, k_out, v_out):
    _kv_expand(ckv_ref[...], krp_ref[...], wuk_ref, wuv_ref, k_out, v_out)


def _cache_kv(ckv, krp, wuk, wuv):
    m = ckv.shape[0]
    full = lambda a: pl.BlockSpec(a.shape, lambda i: (0,) * a.ndim)
    return pl.pallas_call(
        _cache_kv_kernel,
        grid=(1,),
        in_specs=[full(ckv), full(krp), full(wuk), full(wuv)],
        out_specs=[pl.BlockSpec((m, HEADS * LANES), lambda i: (0, 0)),
                   pl.BlockSpec((m, HEADS * MLA_V), lambda i: (0, 0))],
        out_shape=[jax.ShapeDtypeStruct((m, HEADS * LANES), BF16),
                   jax.ShapeDtypeStruct((m, HEADS * MLA_V), BF16)],
        compiler_params=_cparams("arbitrary"),
        name="cache_kv",
    )(ckv, krp, wuk, wuv)


def _head_mask(h):
    lane = lax.broadcasted_iota(jnp.int32, (1, LANES), 1)
    return (lane < 64) if h % 2 == 0 else (lane >= 64)


def _attn_kernel(q_ref, k_ref, v_ref, o_ref, m_scr, l_scr, acc_scr, *, wide, nkv, qscale):
    kv = pl.program_id(2)

    @pl.when(kv == 0)
    def _():
        m_scr[...] = jnp.full(m_scr.shape, -jnp.inf, F32)
        l_scr[...] = jnp.zeros(l_scr.shape, F32)
        acc_scr[...] = jnp.zeros(acc_scr.shape, F32)

    for h in range(HEADS):
        pb = slice((h // 2) * LANES, (h // 2 + 1) * LANES)
        if wide:
            hs = slice(h * LANES, (h + 1) * LANES)
            qh = q_ref[:, hs]
            kh = k_ref[:, hs]
        else:
            qh = jnp.where(_head_mask(h), q_ref[:, pb] * qscale, 0.0).astype(BF16)
            kh = k_ref[:, pb]
        s = lax.dot_general(qh, kh, (((1,), (1,)), ((), ())), preferred_element_type=F32)
        m_prev = m_scr[h]
        m_new = jnp.maximum(m_prev, jnp.max(s, axis=-1, keepdims=True))
        alpha = jnp.exp(m_prev - m_new)
        p = jnp.exp(s - m_new[:, :1])
        l_scr[h] = alpha * l_scr[h] + jnp.sum(p, axis=-1, keepdims=True)
        acc_scr[h] = alpha * acc_scr[h] + jnp.dot(p.astype(BF16), v_ref[:, pb], preferred_element_type=F32)
        m_scr[h] = m_new

    @pl.when(kv == nkv - 1)
    def _():
        for pr in range(HEADS // 2):
            even = acc_scr[2 * pr] / l_scr[2 * pr]
            odd = acc_scr[2 * pr + 1] / l_scr[2 * pr + 1]
            o_ref[:, pr * LANES:(pr + 1) * LANES] = jnp.where(_head_mask(0), even, odd).astype(BF16)


def _attention(q, k, v, *, nb, boff, tq, tk, wide, qscale=1.0):
    s_len, wq = q.shape[1], q.shape[2]
    t_len = k.shape[1]
    nkv = t_len // tk
    kern = functools.partial(_attn_kernel, wide=wide, nkv=nkv, qscale=qscale)
    return pl.pallas_call(
        kern,
        grid=(nb, s_len // tq, nkv),
        in_specs=[
            pl.BlockSpec((None, tq, wq), lambda b, i, j: (b + boff, i, 0)),
            pl.BlockSpec((None, tk, wq), lambda b, i, j: (b, j, 0)),
            pl.BlockSpec((None, tk, 512), lambda b, i, j: (b, j, 0)),
        ],
        out_specs=pl.BlockSpec((None, tq, 512), lambda b, i, j: (b, i, 0)),
        out_shape=jax.ShapeDtypeStruct((nb, s_len, 512), BF16),
        scratch_shapes=[pltpu.VMEM((HEADS, tq, LANES), F32)] * 3,
        compiler_params=_cparams("parallel", "parallel", "arbitrary"),
        name="attn_wide" if wide else "attn_pair",
    )(q, k, v)


def _na_kernel(q_ref, k_ref, v_ref, kc_ref, vc_ref, bias_ref, o_ref):
    i = pl.program_id(1)
    first_row = jnp.clip(NA_QROWS * i - NA_KH // 2, 0, GRID_R - NA_WROWS)
    start = pl.multiple_of(first_row * GRID_W, GRID_W)
    kw = k_ref[pl.ds(start, NA_TW), :]
    vw = v_ref[pl.ds(start, NA_TW), :]
    scale = NA_HD ** -0.5
    nt = (((1,), (1,)), ((), ()))
    outs = []
    for h in range(HEADS):
        pb = slice((h // 2) * LANES, (h // 2 + 1) * LANES)
        qh = jnp.where(_head_mask(h), q_ref[:, pb] * scale, 0.0).astype(BF16)
        s_loc = lax.dot_general(qh, kw[:, pb], nt, preferred_element_type=F32) + bias_ref[h]
        s_ctx = lax.dot_general(qh, kc_ref[:, pb], nt, preferred_element_type=F32)
        m = jnp.maximum(jnp.max(s_loc, axis=-1, keepdims=True), jnp.max(s_ctx, axis=-1, keepdims=True))
        p_loc = jnp.exp(s_loc - m)
        p_ctx = jnp.exp(s_ctx - m)
        l = jnp.sum(p_loc, axis=-1, keepdims=True) + jnp.sum(p_ctx, axis=-1, keepdims=True)
        pv = (jnp.dot(p_loc.astype(BF16), vw[:, pb], preferred_element_type=F32)
              + jnp.dot(p_ctx.astype(BF16), vc_ref[:, pb], preferred_element_type=F32))
        outs.append(pv / l)
    for pr in range(HEADS // 2):
        o_ref[:, pr * LANES:(pr + 1) * LANES] = jnp.where(_head_mask(0), outs[2 * pr], outs[2 * pr + 1]).astype(BF16)


def _na_latent(z3, kc, vc, bias):
    nblk = GRID_R // NA_QROWS

    def cls(b, i):
        return (jnp.where(i == 0, 0, jnp.where(i == nblk - 1, 2, 1)), 0, 0, 0)

    return pl.pallas_call(
        _na_kernel,
        grid=(DEC_BATCH, nblk),
        in_specs=[
            pl.BlockSpec((None, NA_TQ, 512), lambda b, i: (b + 1, i, Z_NAQ // 512)),
            pl.BlockSpec((None, DEC_SEQ, 512), lambda b, i: (b + 1, 0, Z_NAQ // 512 + 1)),
            pl.BlockSpec((None, DEC_SEQ, 512), lambda b, i: (b + 1, 0, Z_NAQ // 512 + 2)),
            pl.BlockSpec((None, PAST_LEN, 512), lambda b, i: (b, 0, 0)),
            pl.BlockSpec((None, PAST_LEN, 512), lambda b, i: (b, 0, 0)),
            pl.BlockSpec((None, HEADS, NA_TQ, NA_TW), cls),
        ],
        out_specs=pl.BlockSpec((None, NA_TQ, 512), lambda b, i: (b, i, 0)),
        out_shape=jax.ShapeDtypeStruct((DEC_BATCH, DEC_SEQ, 512), BF16),
        compiler_params=_cparams("parallel", "arbitrary"),
        name="na_latent",
    )(z3, z3, z3, kc, vc, bias)


def _na_bias_table(rpb):
    a = jnp.arange(NA_QROWS)[:, None, None, None]
    w = jnp.arange(GRID_W)[None, :, None, None]
    j = jnp.arange(NA_WROWS)[None, None, :, None]
    u = jnp.arange(GRID_W)[None, None, None, :]
    col_start = jnp.clip(w - NA_KW // 2, 0, GRID_W - NA_KW)
    col_ok = (u >= col_start) & (u < col_start + NA_KW)
    dc = jnp.clip(u - w, -(NA_KW - 1), NA_KW - 1) + (NA_KW - 1)
    pick_c = jax.nn.one_hot(dc.reshape(GRID_W * GRID_W), 2 * NA_KW - 1, dtype=F32)
    tabs = []
    nblk = GRID_R // NA_QROWS
    for blk in (0, 1, nblk - 1):
        r = NA_QROWS * blk + a
        first = min(max(NA_QROWS * blk - NA_KH // 2, 0), GRID_R - NA_WROWS)
        kr = first + j
        row_start = jnp.clip(r - NA_KH // 2, 0, GRID_R - NA_KH)
        ok = (kr >= row_start) & (kr < row_start + NA_KH) & col_ok
        dr = jnp.clip(kr - r + (NA_KH - 1), 0, 2 * NA_KH - 2)
        pick_r = jax.nn.one_hot(dr.reshape(NA_QROWS * NA_WROWS), 2 * NA_KH - 1, dtype=F32)
        bias = jnp.einsum("xr,hrc,yc->hxy", pick_r, rpb.astype(F32), pick_c, precision=lax.Precision.HIGHEST)
        bias = bias.reshape(HEADS, NA_QROWS, NA_WROWS, GRID_W, GRID_W).transpose(0, 1, 3, 2, 4)
        bias = jnp.where(ok[None], bias, -jnp.inf)
        tabs.append(bias.reshape(HEADS, NA_TQ, NA_TW))
    return jnp.stack(tabs)


CONV_PAD = 16
CONV_CH = 32


def _conv_kernel(zc_ref, wdw_ref, bdw_ref, gln_ref, bln_ref, o_ref, pad_scr, *, seq):
    pad_scr[0:CONV_PAD, :] = jnp.zeros((CONV_PAD, CONV_C), F32)
    pad_scr[CONV_PAD + seq:, :] = jnp.zeros((CONV_PAD, CONV_C), F32)
    glu_ch = 256

    def glu(c, carry):
        r0 = pl.multiple_of(c * glu_ch, glu_ch)
        zc = zc_ref[pl.ds(r0, glu_ch), :].astype(F32)
        pad_scr[pl.ds(CONV_PAD + r0, glu_ch), :] = zc[:, :CONV_C] * jax.nn.sigmoid(zc[:, CONV_C:])
        return carry

    lax.fori_loop(0, seq // glu_ch, glu, 0)

    def conv(c, carry):
        r0 = pl.multiple_of(c * CONV_CH, CONV_CH)
        acc = jnp.zeros((CONV_CH, CONV_C), F32) + bdw_ref[...]
        win = pad_scr[pl.ds(r0, CONV_CH + 2 * CONV_PAD), :]
        for t in range(CONV_K):
            lo = CONV_PAD - CONV_K // 2 + t
            acc = acc + wdw_ref[t:t + 1, :] * win[lo:lo + CONV_CH, :]
        mu = jnp.mean(acc, axis=-1, keepdims=True)
        d = acc - mu
        var = jnp.mean(d * d, axis=-1, keepdims=True)
        y = d * lax.rsqrt(var + EPS) * gln_ref[...] + bln_ref[...]
        o_ref[pl.ds(r0, CONV_CH), :] = (y * jax.nn.sigmoid(y)).astype(BF16)
        return carry

    lax.fori_loop(0, seq // CONV_CH, conv, 0)


def _conv(zv, boff, nseq, seq, w_dw, b_dw, g_ln, b_ln):
    full = lambda a: pl.BlockSpec(a.shape, lambda s: (0,) * a.ndim)
    vec = lambda a: a.reshape(1, CONV_C)
    return pl.pallas_call(
        functools.partial(_conv_kernel, seq=seq),
        grid=(nseq,),
        in_specs=[
            pl.BlockSpec((None, seq, 2 * CONV_C), lambda s: (s + boff, 0, Z_CONV // (2 * CONV_C))),
            full(w_dw), full(vec(b_dw)), full(vec(g_ln)), full(vec(b_ln)),
        ],
        out_specs=pl.BlockSpec((None, seq, CONV_C), lambda s: (s, 0, 0)),
        out_shape=jax.ShapeDtypeStruct((nseq, seq, CONV_C), BF16),
        scratch_shapes=[pltpu.VMEM((seq + 2 * CONV_PAD, CONV_C), F32)],
        compiler_params=_cparams("parallel"),
        name="conv_module",
    )(zv, w_dw, vec(b_dw), vec(g_ln), vec(b_ln))


def _merge_kernel(x_ref, ga_ref, gb_ref, gc_ref, oa_ref, ub_ref, oc_ref, woa_ref, wpw_ref, woc_ref, wout_ref,
                  g1_ref, gn2_ref, sh2_ref, sc2_ref, x_out, h2_out):
    ya = jnp.dot(oa_ref[...], woa_ref[...], preferred_element_type=F32)
    yb = jnp.dot(ub_ref[...], wpw_ref[...], preferred_element_type=F32)
    yc = jnp.dot(oc_ref[...], woc_ref[...], preferred_element_type=F32)
    mix = ga_ref[...].astype(F32) * ya + gb_ref[...].astype(F32) * yb + gc_ref[...].astype(F32) * yc
    x = x_ref[...] + g1_ref[...] * jnp.dot(mix.astype(BF16), wout_ref[...], preferred_element_type=F32)
    x_out[...] = x
    h2_out[...] = (_rms(x, gn2_ref[...]) * (1.0 + sc2_ref[...]) + sh2_ref[...]).astype(BF16)


def _merge(x, z, o_mla, u_conv, o_na, w_o_mla, w_pw2, w_o_na, w_out, mod4, g_norm2):
    tm = 256
    full = lambda a: pl.BlockSpec(a.shape, lambda i: (0,) * a.ndim)
    row = lambda w: pl.BlockSpec((tm, w), lambda i: (i, 0))
    gate = lambda k: pl.BlockSpec((tm, D_MODEL), lambda i: (i, k))
    g_norm2 = g_norm2.reshape(1, D_MODEL)
    return pl.pallas_call(
        _merge_kernel,
        grid=(N_TOK // tm,),
        in_specs=[
            row(D_MODEL), gate(0), gate(1), gate(2), row(512), row(512), row(512),
            full(w_o_mla), full(w_pw2), full(w_o_na), full(w_out),
            _mod_spec(2, tm), full(g_norm2), _mod_spec(3, tm), _mod_spec(4, tm),
        ],
        out_specs=[row(D_MODEL), row(D_MODEL)],
        out_shape=[jax.ShapeDtypeStruct((N_TOK, D_MODEL), F32), jax.ShapeDtypeStruct((N_TOK, D_MODEL), BF16)],
        compiler_params=_cparams("parallel"),
        name="merge_out",
    )(x, z, z, z, o_mla, u_conv, o_na, w_o_mla, w_pw2, w_o_na, w_out, mod4, g_norm2, mod4, mod4)


N_TOP = PEER_TOPK + 1
STAIR = [(a, N_TOP // (a + 1)) for a in range(N_TOP)]
N_CAND = sum(n for _, n in STAIR)
N_CAND_PAD = -(-N_CAND // 8) * 8
N_TOP_PAD = -(-N_TOP // 8) * 8


def _top_rows(s, out_scr):
    for k in range(N_TOP):
        m = jnp.max(s, axis=0, keepdims=True)
        out_scr[k:k + 1, :] = m
        s = jnp.where(s == m, -jnp.inf, s)


def _peer_q_kernel(h2_ref, wpq_ref, k1_ref, k2_ref, thr_ref, s2_ref, w1_ref, w2_ref,
                   v1_scr, v2_scr, cand_scr, sv_scr):
    qp = jnp.dot(h2_ref[...], wpq_ref[...], preferred_element_type=F32).astype(BF16)
    nt = (((1,), (1,)), ((), ()))
    half = PEER_QD // 2
    cand_scr[N_CAND:, :] = jnp.full((N_CAND_PAD - N_CAND, cand_scr.shape[1]), -jnp.inf, F32)
    for h in range(PEER_HEADS):
        q1 = qp[:, h * PEER_QD:h * PEER_QD + half]
        q2 = qp[:, h * PEER_QD + half:(h + 1) * PEER_QD]
        s1 = lax.dot_general(k1_ref[h], q1, nt, preferred_element_type=F32)
        s2 = lax.dot_general(k2_ref[h], q2, nt, preferred_element_type=F32)
        _top_rows(s1, v1_scr)
        _top_rows(s2, v2_scr)
        off = 0
        for a, n in STAIR:
            cand_scr[off:off + n, :] = v1_scr[a:a + 1, :] + v2_scr[0:n, :]
            off += n
        _top_rows(cand_scr[...], sv_scr)
        sv = sv_scr[0:PEER_TOPK, :]
        zsum = jnp.sum(jnp.exp(sv - sv[0:1, :]), axis=0, keepdims=True)
        tau = 0.5 * (sv_scr[PEER_TOPK - 1:PEER_TOPK, :] + sv_scr[PEER_TOPK:PEER_TOPK + 1, :])
        thr_ref[h] = tau - s1
        s2_ref[h] = s2
        w1_ref[h] = jnp.exp(s1 - v1_scr[0:1, :]) * (0.5 / zsum)
        w2_ref[h] = jnp.exp(s2 - v2_scr[0:1, :])


def _peer_q(h2, w_pq, k1, k2):
    tm = 256
    full = lambda a: pl.BlockSpec(a.shape, lambda i: (0,) * a.ndim, pipeline_mode=pl.Buffered(1))
    keyed = pl.BlockSpec((PEER_HEADS, N_KEYS, tm), lambda i: (0, 0, i))
    keyed_shape = jax.ShapeDtypeStruct((PEER_HEADS, N_KEYS, N_TOK), F32)
    return pl.pallas_call(
        _peer_q_kernel,
        grid=(N_TOK // tm,),
        in_specs=[pl.BlockSpec((tm, D_MODEL), lambda i: (i, 0)), full(w_pq), full(k1), full(k2)],
        out_specs=[keyed, keyed, keyed, keyed],
        out_shape=[keyed_shape] * 4,
        scratch_shapes=[pltpu.VMEM((N_TOP_PAD, tm), F32), pltpu.VMEM((N_TOP_PAD, tm), F32),
                        pltpu.VMEM((N_CAND_PAD, tm), F32), pltpu.VMEM((N_TOP_PAD, tm), F32)],
        compiler_params=_cparams("parallel"),
        name="peer_query",
    )(h2, w_pq, k1, k2)


PEER_TM = 512
PEER_E1 = 4
PEER_SB = PEER_E1 * N_KEYS
PEER_NSB = 2
PEER_EB = PEER_NSB * PEER_SB
PEER_FLAGS = None
GELU_C0 = 0.7978845608028654
GELU_C1 = GELU_C0 * 0.044715


GATE_KH = 64
PEER_TH = PEER_TM // 2


def _gate_piece(gate_ref, row0, thr_ref, w1_ref, s2_ref, w2_ref, c, kk):
    cs = slice(c * LANES, (c + 1) * LANES)
    ks = slice(kk * GATE_KH, (kk + 1) * GATE_KH)
    g = [None] * PEER_E1
    for h in range(PEER_HEADS):
        s2 = s2_ref[h, ks, cs]
        w2 = w2_ref[h, ks, cs]
        for e in range(PEER_E1):
            r = row0 + e
            t = jnp.where(s2 >= thr_ref[h, r:r + 1, cs], w2 * w1_ref[h, r:r + 1, cs], 0.0)
            g[e] = t if g[e] is None else g[e] + t
    for e in range(PEER_E1):
        gate_ref[e * N_KEYS + kk * GATE_KH:e * N_KEYS + (kk + 1) * GATE_KH, cs] = g[e]


def _gate_cols(gate_ref, row0, thr_ref, w1_ref, s2_ref, w2_ref, c):
    for kk in range(N_KEYS // GATE_KH):
        _gate_piece(gate_ref, row0, thr_ref, w1_ref, s2_ref, w2_ref, c, kk)


def _peer_kernel(h2t_ref, u_ref, vt_ref, thrc_ref, w1c_ref, thrn_ref, w1n_ref, s2_ref, w2_ref, x_ref, g2_ref,
                 x_out, acc_scr, gate_a, gate_b):
    j = pl.program_id(1)
    gates = (gate_a, gate_b)
    cur = (thrc_ref, w1c_ref, s2_ref, w2_ref)
    nxt = (thrn_ref, w1n_ref, s2_ref, w2_ref)

    @pl.when(j == 0)
    def _():
        acc_scr[...] = jnp.zeros(acc_scr.shape, F32)
        for c in range(PEER_TM // LANES):
            _gate_cols(gate_a, 0, *cur, c)

    def mm1(sb, c2):
        rows = slice(sb * PEER_SB, (sb + 1) * PEER_SB)
        return jnp.dot(u_ref[rows, :], h2t_ref[:, c2 * PEER_TH:(c2 + 1) * PEER_TH], preferred_element_type=F32)

    def act(sb, c2, st):
        a = st * gates[sb][:, c2 * PEER_TH:(c2 + 1) * PEER_TH]
        return (a + a * jnp.tanh(st * (GELU_C0 + GELU_C1 * (st * st)))).astype(BF16)

    def mm2(sb, a):
        return jnp.dot(vt_ref[:, sb * PEER_SB:(sb + 1) * PEER_SB], a, preferred_element_type=F32)

    def acc(c2, y):
        acc_scr[:, c2 * PEER_TH:(c2 + 1) * PEER_TH] += y

    st_a0 = mm1(0, 0)
    _gate_cols(gate_b, PEER_E1, *cur, 0)
    st_a1 = mm1(0, 1)
    _gate_cols(gate_b, PEER_E1, *cur, 1)
    act_a0 = act(0, 0, st_a0)
    st_b0 = mm1(1, 0)
    _gate_cols(gate_b, PEER_E1, *cur, 2)
    act_a1 = act(0, 1, st_a1)
    y_a0 = mm2(0, act_a0)
    _gate_cols(gate_b, PEER_E1, *cur, 3)
    st_b1 = mm1(1, 1)
    _gate_cols(gate_a, 0, *nxt, 0)
    act_b0 = act(1, 0, st_b0)
    y_a1 = mm2(0, act_a1)
    _gate_cols(gate_a, 0, *nxt, 1)
    act_b1 = act(1, 1, st_b1)
    y_b0 = mm2(1, act_b0)
    _gate_cols(gate_a, 0, *nxt, 2)
    acc(0, y_a0 + y_b0)
    y_b1 = mm2(1, act_b1)
    _gate_cols(gate_a, 0, *nxt, 3)
    acc(1, y_a1 + y_b1)

    @pl.when(j == pl.num_programs(1) - 1)
    def _():
        x_out[...] = x_ref[...] + g2_ref[...] * acc_scr[...].T


def _peer(h2t, u, vt, thr, w1, s2, w2, x, mod4):
    tm = PEER_TM
    nj = N_EXPERTS // PEER_EB
    once = pl.Buffered(1)
    e1_cur = pl.BlockSpec((PEER_HEADS, PEER_NSB * PEER_E1, tm), lambda i, j: (0, j, i))
    e1_next = pl.BlockSpec((PEER_HEADS, PEER_NSB * PEER_E1, tm), lambda i, j: (0, jnp.minimum(j + 1, nj - 1), i))
    keyed = pl.BlockSpec((PEER_HEADS, N_KEYS, tm), lambda i, j: (0, 0, i), pipeline_mode=once)
    return pl.pallas_call(
        _peer_kernel,
        grid=(N_TOK // tm, nj),
        in_specs=[
            pl.BlockSpec((D_MODEL, tm), lambda i, j: (0, i), pipeline_mode=once),
            pl.BlockSpec((PEER_EB, D_MODEL), lambda i, j: (j, 0)),
            pl.BlockSpec((D_MODEL, PEER_EB), lambda i, j: (0, j)),
            e1_cur, e1_cur, e1_next, e1_next, keyed, keyed,
            pl.BlockSpec((tm, D_MODEL), lambda i, j: (i, 0), pipeline_mode=once),
            _mod_spec(5, tm),
        ],
        out_specs=pl.BlockSpec((tm, D_MODEL), lambda i, j: (i, 0)),
        out_shape=jax.ShapeDtypeStruct((N_TOK, D_MODEL), F32),
        scratch_shapes=[pltpu.VMEM((D_MODEL, tm), F32), pltpu.VMEM((PEER_SB, tm), F32),
                        pltpu.VMEM((PEER_SB, tm), F32)],
        compiler_params=_cparams("parallel", "arbitrary", flags=PEER_FLAGS),
        name="peer_experts",
    )(h2t, u, vt, thr, w1, thr, w1, s2, w2, x, mod4)


def _final_kernel(x_ref, g_ref, o_ref):
    o_ref[...] = _rms(x_ref[...], g_ref[...])


def _final_norm(x, g):
    tm = 1024
    return pl.pallas_call(
        _final_kernel,
        grid=(N_TOK // tm,),
        in_specs=[pl.BlockSpec((tm, D_MODEL), lambda i: (i, 0)), pl.BlockSpec((1, D_MODEL), lambda i: (0, 0))],
        out_specs=pl.BlockSpec((tm, D_MODEL), lambda i: (i, 0)),
        out_shape=jax.ShapeDtypeStruct((N_TOK, D_MODEL), F32),
        compiler_params=_cparams("parallel"),
        name="final_norm",
    )(x, g.reshape(1, D_MODEL))


def _rot_cols(w):
    q = MLA_ROPE // 4
    return jnp.concatenate([-w[:, q:2 * q], w[:, 0:q], -w[:, 3 * q:4 * q], w[:, 2 * q:3 * q]], axis=1)


def _rope_tables():
    nf = MLA_ROPE // 4
    freqs = 1.0 / (ROPE_BASE ** (jnp.arange(nf, dtype=F32) / nf))
    t = jnp.arange(DEC_SEQ)
    ang_r = (t // GRID_W).astype(F32)[:, None] * freqs[None, :]
    ang_c = (t % GRID_W).astype(F32)[:, None] * freqs[None, :]
    cos32 = jnp.concatenate([jnp.cos(ang_r)] * 2 + [jnp.cos(ang_c)] * 2, axis=1)
    sin32 = jnp.concatenate([jnp.sin(ang_r)] * 2 + [jnp.sin(ang_c)] * 2, axis=1)
    ones = jnp.ones((DEC_SEQ, LANES), F32)
    zeros = jnp.zeros((DEC_SEQ, LANES), F32)
    cos = ones.at[:, MLA_NOPE:MLA_NOPE + MLA_ROPE].set(cos32)
    sin = zeros.at[:, MLA_NOPE:MLA_NOPE + MLA_ROPE].set(sin32)
    return jnp.stack([ones, cos]), jnp.stack([zeros, sin])


def _in_rope_block(w_kr):
    return jnp.zeros((D_MODEL, LANES), F32).at[:, MLA_NOPE:MLA_NOPE + MLA_ROPE].set(w_kr)


def _layout_w_in(w):
    zq, zkv, zkr, zc, zna, zg = jnp.split(w, [512, 768, 800, 1824, 3360], axis=1)
    cat = jnp.concatenate([zg, zna, zq, zc, zkv, _in_rope_block(zkr), _in_rope_block(_rot_cols(zkr))], axis=1)
    return cat.astype(BF16)


def _layout_w_uq(w):
    w = w.reshape(Q_LORA, MLA_HEADS, MLA_NOPE + MLA_ROPE)
    pad = jnp.zeros((Q_LORA, MLA_HEADS, LANES - MLA_NOPE - MLA_ROPE), F32)
    w1 = jnp.concatenate([w, pad], axis=2)
    rot = jnp.stack([_rot_cols(w[:, h, MLA_NOPE:]) for h in range(MLA_HEADS)], axis=1)
    w2 = jnp.concatenate([jnp.zeros((Q_LORA, MLA_HEADS, MLA_NOPE), F32), rot, pad], axis=2)
    return w1.reshape(Q_LORA, -1).astype(BF16), w2.reshape(Q_LORA, -1).astype(BF16)


def _layout_w_uk(w):
    w = w.reshape(KV_LORA, MLA_HEADS, MLA_NOPE)
    w = jnp.concatenate([w, jnp.zeros((KV_LORA, MLA_HEADS, LANES - MLA_NOPE), F32)], axis=2)
    return w.reshape(KV_LORA, -1).astype(BF16)


def kernel(x_prompt, x_sample, cache_ckv, cache_krope, cache_na_k, cache_na_v, c, c_ctx, w_ada, b_ada, g_norm1,
           w_in, g_q_a, w_uq, g_kv_a, w_uk, w_uv, w_o_mla, w_dw, b_dw, g_conv_ln, b_conv_ln, w_pw2, rpb, w_o_na,
           w_out, g_norm2, w_pq, sub_k1, sub_k2, peer_u, peer_v, g_final):
    x = jnp.concatenate([x_prompt.reshape(N_CTX_TOK, D_MODEL), x_sample.reshape(N_LAT_TOK, D_MODEL)], axis=0)
    c8 = jnp.concatenate([c_ctx[None, :], c, jnp.zeros((8 - 1 - DEC_BATCH, D_MODEL), F32)], axis=0)
    mod = _ada(c8, w_ada, b_ada)
    cos_t, sin_t = _rope_tables()
    new_ckv, new_kr, new_nak, new_nav = [], [], [], []
    for l in range(DEPTH):
        mod4 = mod[l].reshape(8, 6, 1, D_MODEL)
        z = _inproj(x, mod4, g_norm1[l], _layout_w_in(w_in[l]))
        z3 = z.reshape(N_GROUPS, GROUP_ROWS, Z_COLS)
        zs = z.reshape(N_TOK // SEQ, SEQ, Z_COLS)

        wq1, wq2 = _layout_w_uq(w_uq[l])
        wuk = _layout_w_uk(w_uk[l])
        wuv = w_uv[l].astype(BF16)
        q, k, v, ckv = _mla_prep(z, cos_t, sin_t, g_q_a[l], wq1, wq2, g_kv_a[l], wuk, wuv)
        krp_c = jnp.zeros((DEC_BATCH * PAST_LEN, LANES), F32).at[:, MLA_NOPE:MLA_NOPE + MLA_ROPE].set(
            cache_krope[:, l].reshape(-1, MLA_ROPE))
        k_c, v_c = _cache_kv(cache_ckv[:, l].reshape(-1, KV_LORA), krp_c, wuk, wuv)
        k3 = k.reshape(N_GROUPS, GROUP_ROWS, -1)
        v3 = v.reshape(N_GROUPS, GROUP_ROWS, -1)
        k_all = jnp.concatenate([k_c.reshape(DEC_BATCH, PAST_LEN, -1), k3[1:]], axis=1)
        v_all = jnp.concatenate([v_c.reshape(DEC_BATCH, PAST_LEN, -1), v3[1:]], axis=1)
        o_lat = _attention(q.reshape(N_GROUPS, GROUP_ROWS, -1), k_all, v_all, nb=DEC_BATCH, boff=1,
                           tq=512, tk=(PAST_LEN + DEC_SEQ) // 2, wide=True)
        o_ctx = _attention(q.reshape(N_TOK // SEQ, SEQ, -1), k.reshape(N_TOK // SEQ, SEQ, -1),
                           v.reshape(N_TOK // SEQ, SEQ, -1), nb=BATCH, boff=0, tq=SEQ, tk=SEQ, wide=True)
        o_mla = jnp.concatenate([o_ctx.reshape(N_CTX_TOK, 512), o_lat.reshape(N_LAT_TOK, 512)], axis=0)

        u_ctx = _conv(zs, 0, BATCH, SEQ, w_dw[l], b_dw[l], g_conv_ln[l], b_conv_ln[l])
        u_lat = _conv(z3, 1, DEC_BATCH, DEC_SEQ, w_dw[l], b_dw[l], g_conv_ln[l], b_conv_ln[l])
        u_conv = jnp.concatenate([u_ctx.reshape(N_CTX_TOK, CONV_C), u_lat.reshape(N_LAT_TOK, CONV_C)], axis=0)

        naq = z[:, Z_NAQ:Z_NAQ + 512].reshape(N_TOK // SEQ, SEQ, 512)
        nak = z[:N_CTX_TOK, Z_NAQ + 512:Z_NAQ + 1024].reshape(BATCH, SEQ, 512)
        nav = z[:N_CTX_TOK, Z_NAQ + 1024:Z_NAQ + 1536].reshape(BATCH, SEQ, 512)
        on_ctx = _attention(naq, nak, nav, nb=BATCH, boff=0, tq=SEQ, tk=SEQ, wide=False, qscale=NA_HD ** -0.5)
        on_lat = _na_latent(z3, cache_na_k[:, l].reshape(DEC_BATCH, PAST_LEN, 512).astype(BF16),
                            cache_na_v[:, l].reshape(DEC_BATCH, PAST_LEN, 512).astype(BF16), _na_bias_table(rpb[l]))
        o_na = jnp.concatenate([on_ctx.reshape(N_CTX_TOK, 512), on_lat.reshape(N_LAT_TOK, 512)], axis=0)

        x, h2 = _merge(x, z, o_mla, u_conv, o_na, w_o_mla[l].astype(BF16), w_pw2[l].astype(BF16),
                       w_o_na[l].astype(BF16), w_out[l].astype(BF16), mod4, g_norm2[l])

        thr, s2, w1, w2 = _peer_q(h2, w_pq[l].astype(BF16), sub_k1[l].astype(BF16), sub_k2[l].astype(BF16))
        x = _peer(h2.T, peer_u[l].astype(BF16), peer_v[l].T.astype(BF16), thr, w1, s2, w2, x, mod4)

        new_ckv.append(ckv[:N_CTX_TOK].reshape(BATCH, SEQ, KV_LORA))
        kr = z[:N_CTX_TOK, Z_TAIL + KV_LORA + MLA_NOPE:Z_TAIL + KV_LORA + MLA_NOPE + MLA_ROPE]
        new_kr.append(kr.astype(F32).reshape(BATCH, SEQ, MLA_ROPE))
        new_nak.append(nak.astype(F32).reshape(BATCH, SEQ, NA_HEADS, NA_HD))
        new_nav.append(nav.astype(F32).reshape(BATCH, SEQ, NA_HEADS, NA_HD))

    y = _final_norm(x, g_final)
    return (y[:N_CTX_TOK].reshape(BATCH, SEQ, D_MODEL), y[N_CTX_TOK:].reshape(DEC_BATCH, DEC_SEQ, D_MODEL),
            jnp.stack(new_ckv, axis=1), jnp.stack(new_kr, axis=1), jnp.stack(new_nak, axis=1),
            jnp.stack(new_nav, axis=1))
```
